```python
import math
import jax, jax.numpy as jnp
from jax import lax
import numpy as np

D_MODEL = 1024
BATCH = 8
SEQ = 4096
DEPTH = 1

ATTN_HEADS = 8
ATTN_HEAD_DIM = 64
ATTN_V_DIM = 2 * ATTN_HEAD_DIM
ROPE_THETA = 500000.0
ROPE_DIM = ATTN_HEAD_DIM // 4
Q_BLOCK = 128
GMLP_WIDTH = D_MODEL
GMLP_GROUPS = 8
GMLP_GROUP_DIM = GMLP_WIDTH // GMLP_GROUPS
GMLP_CHUNK = 128
PEER_HEADS = 8
PEER_KEYS = 128
PEER_EXPERTS = PEER_KEYS * PEER_KEYS
PEER_TOPK = 16
PEER_QDIM = 256
PEER_HALF = PEER_QDIM // 2
PEER_TOKEN_BLOCK = 128
NORM_EPS = 1e-6
N_MOD = 6

Q_COLS = ATTN_HEADS * 2 * ATTN_HEAD_DIM
K_COLS = ATTN_HEADS * 2 * ATTN_HEAD_DIM
V_COLS = ATTN_HEADS * ATTN_V_DIM
Z_COLS = 2 * GMLP_WIDTH
GATE_COLS = 2 * D_MODEL
IN_COLS = Q_COLS + K_COLS + V_COLS + Z_COLS + GATE_COLS
SPLITS = [Q_COLS, Q_COLS + K_COLS, Q_COLS + K_COLS + V_COLS, Q_COLS + K_COLS + V_COLS + Z_COLS]

kernel_name = "hybrid_diffattn_gmlp_peer_encoder"


def rms_norm(x, g):
    xf = x.astype(jnp.float32)
    y = xf * lax.rsqrt(jnp.mean(xf * xf, axis=-1, keepdims=True) + NORM_EPS)
    return (y * g.astype(jnp.float32)).astype(x.dtype)


def layer_norm(x, g, b):
    xf = x.astype(jnp.float32)
    mu = jnp.mean(xf, axis=-1, keepdims=True)
    var = jnp.mean(jnp.square(xf - mu), axis=-1, keepdims=True)
    y = (xf - mu) * lax.rsqrt(var + NORM_EPS)
    return (y * g.astype(jnp.float32) + b.astype(jnp.float32)).astype(x.dtype)


def rope_tables(positions, dtype):
    inv_freq = ROPE_THETA ** (-jnp.arange(0, ROPE_DIM, 2, dtype=jnp.float32) / ROPE_DIM)
    ang = positions.astype(jnp.float32)[..., None] * inv_freq
    return (jnp.cos(ang)[:, :, None, None, :].astype(dtype),
            jnp.sin(ang)[:, :, None, None, :].astype(dtype))


def partial_rope(t, cos, sin):
    half = ROPE_DIM // 2
    r1 = t[..., :half]
    r2 = t[..., half:ROPE_DIM]
    return jnp.concatenate([r1 * cos - r2 * sin, r2 * cos + r1 * sin, t[..., ROPE_DIM:]], axis=-1)


def diff_attention(q, k, v, lam):
    B, S = q.shape[0], q.shape[1]
    nb = S // Q_BLOCK
    scale = ATTN_HEAD_DIM ** -0.5
    qb = (q * scale).reshape(B, nb, Q_BLOCK, ATTN_HEADS, 2, ATTN_HEAD_DIM).transpose(1, 0, 2, 3, 4, 5)

    def block(qi):
        s = jnp.einsum('bqhtd,bkhtd->bhtqk', qi, k).astype(jnp.float32)
        p = jax.nn.softmax(s, axis=-1)
        a = p[:, :, 0] - lam * p[:, :, 1]
        return jnp.einsum('bhqk,bkhe->bqhe', a.astype(v.dtype), v)

    o = lax.map(block, qb)
    return o.transpose(1, 0, 2, 3, 4).reshape(B, S, ATTN_HEADS, ATTN_V_DIM)


def spatial_gating(z, ln_g, ln_b, w_s, b_s):
    u, v = jnp.split(z, 2, axis=-1)
    v = layer_norm(v, ln_g, ln_b)
    B, S = v.shape[0], v.shape[1]
    vc = v.reshape(B, S // GMLP_CHUNK, GMLP_CHUNK, GMLP_GROUPS, GMLP_GROUP_DIM)
    sv = jnp.einsum('gpq,bcqgh->bcpgh', w_s, vc) + b_s.T[None, None, :, :, None]
    return u * sv.reshape(B, S, GMLP_WIDTH)


def peer(h, w_q, sub_keys, exp_u, exp_v):
    B, S, D = h.shape
    q = (h @ w_q).reshape(B, S, PEER_HEADS, 2, PEER_HALF)
    s = jnp.einsum('bshtd,htnd->bshtn', q, sub_keys).astype(jnp.float32)
    s_top, i_top = lax.top_k(s, PEER_TOPK)
    cand = s_top[..., 0, :, None] + s_top[..., 1, None, :]
    cand_idx = i_top[..., 0, :, None] * PEER_KEYS + i_top[..., 1, None, :]
    cand = cand.reshape(B, S, PEER_HEADS, PEER_TOPK * PEER_TOPK)
    cand_idx = cand_idx.reshape(B, S, PEER_HEADS, PEER_TOPK * PEER_TOPK)
    best, sel = lax.top_k(cand, PEER_TOPK)
    idx = jnp.take_along_axis(cand_idx, sel, axis=-1)
    g = jax.nn.softmax(best, axis=-1).astype(h.dtype)
    T = B * S
    E = PEER_HEADS * PEER_TOPK
    nb = T // PEER_TOKEN_BLOCK
    xt = h.reshape(nb, PEER_TOKEN_BLOCK, D)
    it = idx.reshape(nb, PEER_TOKEN_BLOCK, E)
    gt = g.reshape(nb, PEER_TOKEN_BLOCK, E)

    def block(args):
        xb, ib, gb = args
        u = exp_u[ib]
        a = jax.nn.gelu(jnp.einsum('td,ted->te', xb, u)) * gb
        return jnp.einsum('te,ted->td', a, exp_v[ib])

    y = lax.map(block, (xt, it, gt))
    return y.reshape(B, S, D)


def setup_inputs(seed: int = 0) -> dict:
    key = jax.random.key(seed)
    ks = jax.random.split(key, 32)
    f32 = jnp.float32
    L, D = DEPTH, D_MODEL

    def nrm(k, shape, std):
        return jax.random.normal(k, shape, f32) * std

    def gain(k, shape):
        return 1.0 + 0.02 * jax.random.normal(k, shape, f32)

    x = jax.random.normal(ks[0], (BATCH, SEQ, D), f32)
    c = jax.random.normal(ks[1], (BATCH, D), f32)
    offset = jax.random.randint(ks[2], (BATCH, 1), 0, 1024, dtype=jnp.int32)
    positions = jnp.arange(SEQ, dtype=jnp.int32)[None, :] + offset
    return {
        "x": x,
        "c": c,
        "positions": positions,
        "w_ada": nrm(ks[3], (L, D, N_MOD * D), 0.5 * D ** -0.5),
        "b_ada": nrm(ks[4], (L, N_MOD * D), 0.02),
        "g_pre_mix": gain(ks[5], (L, D)),
        "g_post_mix": gain(ks[6], (L, D)),
        "w_in": nrm(ks[7], (L, D, IN_COLS), D ** -0.5),
        "lambda_q1": nrm(ks[8], (L, ATTN_HEAD_DIM), 0.1),
        "lambda_k1": nrm(ks[9], (L, ATTN_HEAD_DIM), 0.1),
        "lambda_q2": nrm(ks[10], (L, ATTN_HEAD_DIM), 0.1),
        "lambda_k2": nrm(ks[11], (L, ATTN_HEAD_DIM), 0.1),
        "g_subln": gain(ks[12], (L, ATTN_V_DIM)),
        "w_attn_branch": nrm(ks[13], (L, V_COLS, D), V_COLS ** -0.5),
        "gmlp_ln_g": gain(ks[14], (L, GMLP_WIDTH)),
        "gmlp_ln_b": nrm(ks[15], (L, GMLP_WIDTH), 0.02),
        "w_spatial": nrm(ks[16], (L, GMLP_GROUPS, GMLP_CHUNK, GMLP_CHUNK), GMLP_CHUNK ** -0.5),
        "b_spatial": gain(ks[17], (L, GMLP_GROUPS, GMLP_CHUNK)),
        "w_gmlp_branch": nrm(ks[18], (L, GMLP_WIDTH, D), GMLP_WIDTH ** -0.5),
        "w_out": nrm(ks[19], (L, D, D), D ** -0.5),
        "g_pre_ffn": gain(ks[20], (L, D)),
        "g_post_ffn": gain(ks[21], (L, D)),
        "w_peer_q": nrm(ks[22], (L, D, PEER_HEADS * PEER_QDIM), D ** -0.5),
        "peer_sub_keys": nrm(ks[23], (L, PEER_HEADS, 2, PEER_KEYS, PEER_HALF), PEER_HALF ** -0.5),
        "peer_u": nrm(ks[24], (L, PEER_EXPERTS, D), D ** -0.5),
        "peer_v": nrm(ks[25], (L, PEER_EXPERTS, D), PEER_HEADS ** -0.5),
    }


def reference(x, c, positions, w_ada, b_ada, g_pre_mix, g_post_mix, w_in,
              lambda_q1, lambda_k1, lambda_q2, lambda_k2, g_subln, w_attn_branch,
              gmlp_ln_g, gmlp_ln_b, w_spatial, b_spatial, w_gmlp_branch, w_out,
              g_pre_ffn, g_post_ffn, w_peer_q, peer_sub_keys, peer_u, peer_v):
    B, S, D = x.shape
    cos, sin = rope_tables(positions, x.dtype)
    c_act = jax.nn.silu(c)
    for l in range(DEPTH):
        mod = c_act @ w_ada[l] + b_ada[l]
        sh1, sc1, gt1, sh2, sc2, gt2 = [m[:, None, :] for m in jnp.split(mod, N_MOD, axis=-1)]

        h = rms_norm(x, g_pre_mix[l]) * (1.0 + sc1) + sh1
        p = h @ w_in[l]
        q, k, v, z, gates = jnp.split(p, SPLITS, axis=-1)

        q = partial_rope(q.reshape(B, S, ATTN_HEADS, 2, ATTN_HEAD_DIM), cos, sin)
        k = partial_rope(k.reshape(B, S, ATTN_HEADS, 2, ATTN_HEAD_DIM), cos, sin)
        v = v.reshape(B, S, ATTN_HEADS, ATTN_V_DIM)
        lambda_init = 0.8 - 0.6 * math.exp(-0.3 * l)
        lam = (jnp.exp(jnp.sum(lambda_q1[l].astype(jnp.float32) * lambda_k1[l].astype(jnp.float32)))
               - jnp.exp(jnp.sum(lambda_q2[l].astype(jnp.float32) * lambda_k2[l].astype(jnp.float32)))
               + lambda_init)
        o = diff_attention(q, k, v, lam)
        o = rms_norm(o, g_subln[l]) * (1.0 - lambda_init)
        y_attn = o.reshape(B, S, V_COLS) @ w_attn_branch[l]

        sg = spatial_gating(jax.nn.gelu(z), gmlp_ln_g[l], gmlp_ln_b[l], w_spatial[l], b_spatial[l])
        y_gmlp = sg @ w_gmlp_branch[l]

        gate_a, gate_g = jnp.split(jax.nn.sigmoid(gates), 2, axis=-1)
        mix = (gate_a * y_attn + gate_g * y_gmlp) @ w_out[l]
        x = x + gt1 * rms_norm(mix, g_post_mix[l])

        h2 = rms_norm(x, g_pre_ffn[l]) * (1.0 + sc2) + sh2
        y = peer(h2, w_peer_q[l], peer_sub_keys[l], peer_u[l], peer_v[l])
        x = x + gt2 * rms_norm(y, g_post_ffn[l])
    return x
```

```python
import functools
import math

import jax
import jax.numpy as jnp
from jax import lax
from jax.experimental import pallas as pl
from jax.experimental.pallas import tpu as pltpu

F32 = jnp.float32
BF16 = jnp.bfloat16

ATTN_HEADS = 8
HEAD_DIM = 64
V_DIM = 2 * HEAD_DIM
ROPE_HALF = 8
ROPE_THETA = 500000.0
GMLP_GROUPS = 8
GMLP_CHUNK = 128
PEER_HEADS = 8
PEER_KEYS = 128
PEER_TOPK = 16
PEER_HALF = 128
NORM_EPS = 1e-6
N_MOD = 6
LAMBDA_INIT = 0.8 - 0.6 * math.exp(-0.3 * 0)
UNRANKED = 64.0

LANES = 128
VMEM_LIMIT = 56 * 1024 * 1024

NT_DIMS = (((1,), (1,)), ((), ()))


def _params(sem):
    return pltpu.CompilerParams(dimension_semantics=sem, vmem_limit_bytes=VMEM_LIMIT)


def _gelu_tanh(x):
    return 0.5 * x * (1.0 + jnp.tanh(0.7978845608028654 * (x + 0.044715 * (x * x * x))))


def _sigmoid(x):
    return 1.0 / (1.0 + jnp.exp(-x))


def _rms(x, axis):
    return x * lax.rsqrt(jnp.mean(x * x, axis=axis, keepdims=True) + NORM_EPS)


def _mod_kernel(c_ref, w_ref, b_ref, o_ref):
    c = c_ref[...]
    ca = c * _sigmoid(c)
    o_ref[...] = jnp.dot(ca, w_ref[...], preferred_element_type=F32,
                         precision=lax.Precision.HIGHEST) + b_ref[...]


def _mod_call(c, w_ada, b_ada):
    B, D = c.shape
    return pl.pallas_call(
        _mod_kernel,
        grid=(N_MOD,),
        in_specs=[pl.BlockSpec((B, D), lambda j: (0, 0)),
                  pl.BlockSpec((D, D), lambda j: (0, j)),
                  pl.BlockSpec((1, D), lambda j: (0, j))],
        out_specs=pl.BlockSpec((B, D), lambda j: (0, j)),
        out_shape=jax.ShapeDtypeStruct((B, N_MOD * D), F32),
        compiler_params=_params(("arbitrary",)),
        name="mod",
    )(c, w_ada, b_ada.reshape(1, N_MOD * D))


def _prenorm(x_ref, g_ref, sc_ref, sh_ref):
    x = x_ref[0]
    return (_rms(x, -1) * g_ref[...] * (1.0 + sc_ref[0]) + sh_ref[0]).astype(BF16)


def _qkv_kernel(x_ref, g_ref, sc_ref, sh_ref, wq_ref, wk_ref, wv_ref, cos_ref, sin_ref,
                kc_ref, ka_ref, kb_ref, q_out, k_out, v_out):
    hb = _prenorm(x_ref, g_ref, sc_ref, sh_ref)
    qt = lax.dot_general(wq_ref[...], hb, NT_DIMS, preferred_element_type=F32)
    cos = cos_ref[0]
    sin = sin_ref[0]
    pieces = []
    for g in range(2 * ATTN_HEADS):
        base = g * HEAD_DIM
        r1 = qt[base:base + ROPE_HALF]
        r2 = qt[base + ROPE_HALF:base + 2 * ROPE_HALF]
        pieces += [r1 * cos - r2 * sin, r2 * cos + r1 * sin, qt[base + 2 * ROPE_HALF:base + HEAD_DIM]]
    q_out[0] = (jnp.concatenate(pieces, axis=0) * (HEAD_DIM ** -0.5)).astype(BF16)
    kf = jnp.dot(hb, wk_ref[...], preferred_element_type=F32)
    kc, ka, kb = kc_ref[0], ka_ref[0], kb_ref[0]
    outs = []
    for c in range(ATTN_HEADS):
        t = kf[:, c * LANES:(c + 1) * LANES]
        outs.append(t * kc + pltpu.roll(t, ROPE_HALF, 1) * ka + pltpu.roll(t, LANES - ROPE_HALF, 1) * kb)
    k_out[0] = jnp.concatenate(outs, axis=1).astype(BF16)
    v_out[0] = lax.dot_general(wv_ref[...], hb, NT_DIMS, preferred_element_type=F32).astype(BF16)


def _qkv_call(x, g_pre, sc1, sh1, wq_t, wk, wv_t, cos_t, sin_t, kc, ka, kb, tm):
    B, S, D = x.shape
    C = ATTN_HEADS * V_DIM
    tok = lambda b, i: (b, i, 0)
    per_b = lambda b, i: (b, 0, 0)
    const = lambda b, i: (0, 0)
    tr = lambda b, i: (b, 0, i)
    return pl.pallas_call(
        _qkv_kernel,
        grid=(B, S // tm),
        in_specs=[pl.BlockSpec((1, tm, D), tok),
                  pl.BlockSpec((1, D), const),
                  pl.BlockSpec((1, 1, D), per_b),
                  pl.BlockSpec((1, 1, D), per_b),
                  pl.BlockSpec((C, D), const),
                  pl.BlockSpec((D, C), const),
                  pl.BlockSpec((C, D), const),
                  pl.BlockSpec((1, ROPE_HALF, tm), tr),
                  pl.BlockSpec((1, ROPE_HALF, tm), tr),
                  pl.BlockSpec((1, tm, LANES), tok),
                  pl.BlockSpec((1, tm, LANES), tok),
                  pl.BlockSpec((1, tm, LANES), tok)],
        out_specs=[pl.BlockSpec((1, C, tm), tr),
                   pl.BlockSpec((1, tm, C), tok),
                   pl.BlockSpec((1, C, tm), tr)],
        out_shape=[jax.ShapeDtypeStruct((B, C, S), BF16),
                   jax.ShapeDtypeStruct((B, S, C), BF16),
                   jax.ShapeDtypeStruct((B, C, S), BF16)],
        compiler_params=_params(("parallel", "parallel")),
        name="qkv",
    )(x, g_pre, sc1, sh1, wq_t, wk, wv_t, cos_t, sin_t, kc, ka, kb)


def _gmlp_kernel(x_ref, g_ref, sc_ref, sh_ref, wz_ref, wg_ref, lng_ref, lnb_ref, ws_ref, bs_ref,
                 wo_ref, ga_out, gy_out):
    hb = _prenorm(x_ref, g_ref, sc_ref, sh_ref)
    tm = hb.shape[0]
    W = lng_ref.shape[1]
    gz = _gelu_tanh(jnp.dot(hb, wz_ref[...], preferred_element_type=F32))
    u = gz[:, :W]
    v = gz[:, W:]
    mu = jnp.mean(v, axis=-1, keepdims=True)
    vc = v - mu
    var = jnp.mean(vc * vc, axis=-1, keepdims=True)
    vn = (vc * lax.rsqrt(var + NORM_EPS) * lng_ref[...] + lnb_ref[...]).astype(BF16)
    gd = W // GMLP_GROUPS
    rows = []
    for ch in range(tm // GMLP_CHUNK):
        r0 = ch * GMLP_CHUNK
        cols = [jnp.dot(ws_ref[g], vn[r0:r0 + GMLP_CHUNK, g * gd:(g + 1) * gd],
                        preferred_element_type=F32) for g in range(GMLP_GROUPS)]
        rows.append(jnp.concatenate(cols, axis=1) + bs_ref[...])
    sv = jnp.concatenate(rows, axis=0)
    sg = (u * sv).astype(BF16)
    yg = jnp.dot(sg, wo_ref[...], preferred_element_type=F32)
    gates = _sigmoid(jnp.dot(hb, wg_ref[...], preferred_element_type=F32))
    D = yg.shape[1]
    ga_out[0] = gates[:, :D].astype(BF16)
    gy_out[0] = (gates[:, D:] * yg).astype(BF16)


def _gmlp_call(x, g_pre, sc1, sh1, wz, wg, ln_g, ln_b, ws, bs, wo, tm):
    B, S, D = x.shape
    W = ln_g.shape[1]
    tok = lambda b, i: (b, i, 0)
    per_b = lambda b, i: (b, 0, 0)
    const = lambda b, i: (0, 0)
    return pl.pallas_call(
        _gmlp_kernel,
        grid=(B, S // tm),
        in_specs=[pl.BlockSpec((1, tm, D), tok),
                  pl.BlockSpec((1, D), const),
                  pl.BlockSpec((1, 1, D), per_b),
                  pl.BlockSpec((1, 1, D), per_b),
                  pl.BlockSpec((D, 2 * W), const),
                  pl.BlockSpec((D, 2 * D), const),
                  pl.BlockSpec((1, W), const),
                  pl.BlockSpec((1, W), const),
                  pl.BlockSpec((GMLP_GROUPS, GMLP_CHUNK, GMLP_CHUNK), lambda b, i: (0, 0, 0)),
                  pl.BlockSpec((GMLP_CHUNK, W), const),
                  pl.BlockSpec((W, D), const)],
        out_specs=[pl.BlockSpec((1, tm, D), tok), pl.BlockSpec((1, tm, D), tok)],
        out_shape=[jax.ShapeDtypeStruct((B, S, D), BF16), jax.ShapeDtypeStruct((B, S, D), BF16)],
        compiler_params=_params(("parallel", "parallel")),
        name="gmlp",
    )(x, g_pre, sc1, sh1, wz, wg, ln_g, ln_b, ws, bs, wo)


def _attn_kernel(lam_ref, q_ref, k_ref, v_ref, gs_ref, o_ref, acc1_ref, acc2_ref, *, tk):
    S = k_ref.shape[1]
    tq = q_ref.shape[2]
    lp = lam_ref[...]
    lam = (jnp.exp(jnp.sum(lp[0:1] * lp[1:2], axis=1, keepdims=True))
           - jnp.exp(jnp.sum(lp[2:3] * lp[3:4], axis=1, keepdims=True)) + LAMBDA_INIT)
    q = q_ref[0].astype(F32)
    row = lax.broadcasted_iota(jnp.int32, q.shape, 0)
    q1 = jnp.where(row < HEAD_DIM, q, 0.0).astype(BF16)
    q2 = jnp.where(row >= HEAD_DIM, q, 0.0).astype(BF16)
    acc1_ref[...] = jnp.zeros_like(acc1_ref)
    acc2_ref[...] = jnp.zeros_like(acc2_ref)

    def step(qz, acc_ref, kb, vb, m, l):
        s = jnp.dot(kb, qz, preferred_element_type=F32)
        mn = jnp.maximum(m, jnp.max(s, axis=0, keepdims=True))
        alpha = jnp.exp(m - mn)
        p = jnp.exp(s - mn)
        l = alpha * l + jnp.sum(p, axis=0, keepdims=True)
        acc_ref[...] = alpha * acc_ref[...] + jnp.dot(vb, p.astype(BF16), preferred_element_type=F32)
        return mn, l

    def body(j, carry):
        m1, l1, m2, l2 = carry
        off = pl.multiple_of(j * tk, tk)
        kb = k_ref[0, pl.ds(off, tk), :]
        vb = v_ref[0, :, pl.ds(off, tk)]
        m1, l1 = step(q1, acc1_ref, kb, vb, m1, l1)
        m2, l2 = step(q2, acc2_ref, kb, vb, m2, l2)
        return m1, l1, m2, l2

    ninf = jnp.full((1, tq), -jnp.inf, F32)
    zero = jnp.zeros((1, tq), F32)
    _, l1, _, l2 = lax.fori_loop(0, S // tk, body, (ninf, zero, ninf, zero))
    o = acc1_ref[...] / l1 - lam * (acc2_ref[...] / l2)
    on = _rms(o, 0) * gs_ref[...] * (1.0 - LAMBDA_INIT)
    o_ref[0] = on.T.astype(BF16)


def _attn_call(lam_p, q_t, k, v_t, g_subln, tq, tk):
    B, S, C = k.shape
    return pl.pallas_call(
        functools.partial(_attn_kernel, tk=tk),
        grid=(B, ATTN_HEADS, S // tq),
        in_specs=[pl.BlockSpec((4, HEAD_DIM), lambda b, h, i: (0, 0)),
                  pl.BlockSpec((1, V_DIM, tq), lambda b, h, i: (b, h, i)),
                  pl.BlockSpec((1, S, V_DIM), lambda b, h, i: (b, 0, h)),
                  pl.BlockSpec((1, V_DIM, S), lambda b, h, i: (b, h, 0)),
                  pl.BlockSpec((V_DIM, 1), lambda b, h, i: (0, 0))],
        out_specs=pl.BlockSpec((1, tq, V_DIM), lambda b, h, i: (b, i, h)),
        out_shape=jax.ShapeDtypeStruct((B, S, C), BF16),
        scratch_shapes=[pltpu.VMEM((V_DIM, tq), F32), pltpu.VMEM((V_DIM, tq), F32)],
        compiler_params=_params(("parallel", "parallel", "parallel")),
        name="attn",
    )(lam_p, q_t, k, v_t, g_subln)


def _merge_kernel(o_ref, ga_ref, gy_ref, x_ref, gt1_ref, sc2_ref, sh2_ref, gpm_ref, gpf_ref,
                  wa_ref, wo_ref, x1_out, h2_out):
    ya = jnp.dot(o_ref[0], wa_ref[...], preferred_element_type=F32)
    mixin = (ga_ref[0].astype(F32) * ya + gy_ref[0].astype(F32)).astype(BF16)
    mix = jnp.dot(mixin, wo_ref[...], preferred_element_type=F32)
    x1 = x_ref[0] + gt1_ref[0] * (_rms(mix, -1) * gpm_ref[...])
    x1_out[0] = x1
    h2_out[0] = (_rms(x1, -1) * gpf_ref[...] * (1.0 + sc2_ref[0]) + sh2_ref[0]).astype(BF16)


def _merge_call(o, ga, gy, x, gt1, sc2, sh2, g_post_mix, g_pre_ffn, wa, wo, tm):
    B, S, D = x.shape
    C = o.shape[2]
    tok = lambda b, i: (b, i, 0)
    per_b = lambda b, i: (b, 0, 0)
    const = lambda b, i: (0, 0)
    return pl.pallas_call(
        _merge_kernel,
        grid=(B, S // tm),
        in_specs=[pl.BlockSpec((1, tm, C), tok),
                  pl.BlockSpec((1, tm, D), tok),
                  pl.BlockSpec((1, tm, D), tok),
                  pl.BlockSpec((1, tm, D), tok),
                  pl.BlockSpec((1, 1, D), per_b),
                  pl.BlockSpec((1, 1, D), per_b),
                  pl.BlockSpec((1, 1, D), per_b),
                  pl.BlockSpec((1, D), const),
                  pl.BlockSpec((1, D), const),
                  pl.BlockSpec((C, D), const),
                  pl.BlockSpec((D, D), const)],
        out_specs=[pl.BlockSpec((1, tm, D), tok), pl.BlockSpec((1, tm, D), tok)],
        out_shape=[jax.ShapeDtypeStruct((B, S, D), F32), jax.ShapeDtypeStruct((B, S, D), BF16)],
        compiler_params=_params(("parallel", "parallel")),
        name="merge",
    )(o, ga, gy, x, gt1, sc2, sh2, g_post_mix, g_pre_ffn, wa, wo)


def _route_kernel(h2_ref, wq_ref, keys_ref, rb_out, e2_out, cnt_out, e1_out, s_ref):
    tm = h2_ref.shape[0]
    qt = lax.dot_general(wq_ref[...], h2_ref[...], NT_DIMS, preferred_element_type=F32).astype(BF16)
    for c in range(2 * PEER_HEADS):
        s_ref[c] = jnp.dot(keys_ref[c], qt[c * PEER_HALF:(c + 1) * PEER_HALF],
                           preferred_element_type=F32)

    K = PEER_TOPK
    row_k = lax.broadcasted_iota(jnp.int32, (K, LANES), 0)
    row_8 = lax.broadcasted_iota(jnp.int32, (8, LANES), 0)
    ninf = -jnp.inf

    for lb in range(tm // LANES):
        lanes = slice(lb * LANES, (lb + 1) * LANES)

        def head_body(h, carry, lanes=lanes):
            s1 = s_ref[2 * h, :, lanes]
            s2 = s_ref[2 * h + 1, :, lanes]

            def extract(r, st):
                w1, w2, k1, k2, v1, v2 = st
                rf = r.astype(F32)
                m1 = jnp.max(w1, axis=0, keepdims=True)
                m2 = jnp.max(w2, axis=0, keepdims=True)
                hit1 = w1 == m1
                hit2 = w2 == m2
                return (jnp.where(hit1, ninf, w1), jnp.where(hit2, ninf, w2),
                        jnp.where(hit1, rf, k1), jnp.where(hit2, rf, k2),
                        jnp.where(row_k == r, m1, v1), jnp.where(row_k == r, m2, v2))

            unr = jnp.full((PEER_KEYS, LANES), UNRANKED, F32)
            zk = jnp.zeros((K, LANES), F32)
            _, _, k1, k2, v1, v2 = lax.fori_loop(0, K, extract, (s1, s2, unr, unr, zk, zk))

            parts = [v1 + v2[0:1]]
            for b in range(1, 8):
                lim = K // (b + 1)
                parts.append(jnp.where(row_8 < lim, v1[0:8] + v2[b:b + 1], ninf))
            parts.append(v1[0:1] + v2[8:16])
            cand = jnp.concatenate(parts, axis=0)

            def peel(r, st):
                w, _ = st
                m = jnp.max(w, axis=0, keepdims=True)
                return jnp.where(w == m, ninf, w), m

            _, tau = lax.fori_loop(0, K, peel, (cand, jnp.zeros((1, LANES), F32)))
            sel = cand >= tau
            top = v1[0:1] + v2[0:1]
            z = jnp.sum(jnp.where(sel, jnp.exp(cand - top), 0.0), axis=0, keepdims=True)
            self_ = sel.astype(F32)
            cb = self_[16:24]
            for b in range(2, 8):
                cb = cb + self_[8 + 8 * b:16 + 8 * b]
            cnt_a = self_[0:16] + jnp.concatenate([cb, jnp.zeros((8, LANES), F32)], axis=0)
            cnt_a = cnt_a + jnp.where(row_k == 0, jnp.sum(self_[72:80], axis=0, keepdims=True), 0.0)
            cnt = jnp.zeros((PEER_KEYS, LANES), F32)
            for a in range(K):
                cnt = jnp.where(k1 == float(a), cnt_a[a:a + 1], cnt)
            rb_out[h, :, lanes] = k2
            e2_out[h, :, lanes] = jnp.exp(s2 - v2[0:1])
            cnt_out[h, :, lanes] = cnt
            e1_out[h, :, lanes] = jnp.exp(s1 - v1[0:1]) / z
            return carry

        lax.fori_loop(0, PEER_HEADS, head_body, 0)


def _route_call(h2, wq_t, keys, tm):
    T, D = h2.shape
    Q = wq_t.shape[0]
    shp = jax.ShapeDtypeStruct((PEER_HEADS, PEER_KEYS, T), F32)
    ospec = pl.BlockSpec((PEER_HEADS, PEER_KEYS, tm), lambda i: (0, 0, i))
    return pl.pallas_call(
        _route_kernel,
        grid=(T // tm,),
        in_specs=[pl.BlockSpec((tm, D), lambda i: (i, 0)),
                  pl.BlockSpec((Q, D), lambda i: (0, 0)),
                  pl.BlockSpec((2 * PEER_HEADS, PEER_KEYS, PEER_HALF), lambda i: (0, 0, 0))],
        out_specs=[ospec, ospec, ospec, ospec],
        out_shape=[shp, shp, shp, shp],
        scratch_shapes=[pltpu.VMEM((2 * PEER_HEADS, PEER_KEYS, tm), F32)],
        compiler_params=_params(("parallel",)),
        name="route",
    )(h2, wq_t, keys)


def _peer_kernel(h2_ref, u_ref, vt_ref, rb_ref, e2_ref, cnt_ref, e1_ref, x1_ref, gt2_ref, g_ref,
                 o_ref, acc_ref, *, ni):
    e = pl.program_id(1)

    @pl.when(e == 0)
    def _():
        acc_ref[...] = jnp.zeros_like(acc_ref)

    tm = h2_ref.shape[0]
    at = lax.dot_general(u_ref[...], h2_ref[...], NT_DIMS, preferred_element_type=F32)
    act = _gelu_tanh(at)
    blocks = []
    for ii in range(ni):
        i = e * ni + ii
        g = jnp.zeros((PEER_KEYS, tm), F32)
        for h in range(PEER_HEADS):
            c = cnt_ref[h, pl.ds(i, 1), :]
            w1 = e1_ref[h, pl.ds(i, 1), :]
            g = g + jnp.where(rb_ref[h] < c, e2_ref[h], 0.0) * w1
        blocks.append(g)
    gate = jnp.concatenate(blocks, axis=0)
    w = (act * gate).astype(BF16)
    acc_ref[...] += jnp.dot(vt_ref[...], w, preferred_element_type=F32)

    @pl.when(e == pl.num_programs(1) - 1)
    def _():
        yn = _rms(acc_ref[...], 0) * g_ref[...]
        o_ref[...] = x1_ref[...] + gt2_ref[0] * yn.T


def _peer_call(h2, u, v_t, rb, e2, cnt, e1, x1, gt2, g_post, tm, te, S):
    T, D = h2.shape
    E = u.shape[0]
    ni = te // PEER_KEYS
    rspec = pl.BlockSpec((PEER_HEADS, PEER_KEYS, tm), lambda t, e: (0, 0, t))
    return pl.pallas_call(
        functools.partial(_peer_kernel, ni=ni),
        grid=(T // tm, E // te),
        in_specs=[pl.BlockSpec((tm, D), lambda t, e: (t, 0)),
                  pl.BlockSpec((te, D), lambda t, e: (e, 0)),
                  pl.BlockSpec((D, te), lambda t, e: (0, e)),
                  rspec, rspec, rspec, rspec,
                  pl.BlockSpec((tm, D), lambda t, e: (t, 0)),
                  pl.BlockSpec((1, 1, D), lambda t, e: ((t * tm) // S, 0, 0)),
                  pl.BlockSpec((D, 1), lambda t, e: (0, 0))],
        out_specs=pl.BlockSpec((tm, D), lambda t, e: (t, 0)),
        out_shape=jax.ShapeDtypeStruct((T, D), F32),
        scratch_shapes=[pltpu.VMEM((D, tm), F32)],
        compiler_params=_params(("parallel", "arbitrary")),
        name="peer",
    )(h2, u, v_t, rb, e2, cnt, e1, x1, gt2, g_post)


def _rope_tables(positions):
    inv_freq = ROPE_THETA ** (-jnp.arange(0, 2 * ROPE_HALF, 2, dtype=F32) / (2 * ROPE_HALF))
    ang = positions.astype(F32)[..., None] * inv_freq
    cos, sin = jnp.cos(ang), jnp.sin(ang)
    lane = jnp.arange(LANES)
    d = lane % HEAD_DIM
    f = d % ROPE_HALF
    kc = jnp.where(d < 2 * ROPE_HALF, cos[..., f], 1.0)
    ka = jnp.where((d >= ROPE_HALF) & (d < 2 * ROPE_HALF), sin[..., f], 0.0)
    kb = jnp.where(d < ROPE_HALF, -sin[..., f], 0.0)
    return cos.transpose(0, 2, 1), sin.transpose(0, 2, 1), kc, ka, kb


def _tile(n, pref):
    t = min(n, pref)
    assert n % t == 0, (n, t)
    return t


def kernel(x, c, positions, w_ada, b_ada, g_pre_mix, g_post_mix, w_in, lambda_q1, lambda_k1, lambda_q2, lambda_k2, g_subln, w_attn_branch, gmlp_ln_g, gmlp_ln_b, w_spatial, b_spatial, w_gmlp_branch, w_out, g_pre_ffn, g_post_ffn, w_peer_q, peer_sub_keys, peer_u, peer_v):
    B, S, D = x.shape
    depth = w_ada.shape[0]
    assert depth == 1 and S % GMLP_CHUNK == 0
    C = ATTN_HEADS * V_DIM
    W = gmlp_ln_g.shape[1]
    T = B * S
    cos_t, sin_t, kc, ka, kb = _rope_tables(positions)
    l = 0
    mod = _mod_call(c, w_ada[l], b_ada[l])
    sh1, sc1, gt1, sh2, sc2, gt2 = [m[:, None, :] for m in jnp.split(mod, N_MOD, axis=-1)]

    wi = w_in[l].astype(BF16)
    wq_t = wi[:, :C].T
    wk = wi[:, C:2 * C]
    wv_t = wi[:, 2 * C:3 * C].T
    wz = wi[:, 3 * C:3 * C + 2 * W]
    wg = wi[:, 3 * C + 2 * W:]
    row = lambda a: a.reshape(1, -1)

    tm = _tile(S, 256)
    q_t, k, v_t = _qkv_call(x, row(g_pre_mix[l]), sc1, sh1, wq_t, wk, wv_t, cos_t, sin_t, kc, ka, kb, tm)
    bs = jnp.repeat(b_spatial[l].T, W // GMLP_GROUPS, axis=1)
    ga, gy = _gmlp_call(x, row(g_pre_mix[l]), sc1, sh1, wz, wg, row(gmlp_ln_g[l]), row(gmlp_ln_b[l]),
                        w_spatial[l].astype(BF16), bs, w_gmlp_branch[l].astype(BF16), tm)

    lam_p = jnp.stack([lambda_q1[l], lambda_k1[l], lambda_q2[l], lambda_k2[l]])
    o = _attn_call(lam_p, q_t, k, v_t, g_subln[l].reshape(V_DIM, 1), _tile(S, 512), _tile(S, 512))

    x1, h2 = _merge_call(o, ga, gy, x, gt1, sc2, sh2, row(g_post_mix[l]), row(g_pre_ffn[l]),
                         w_attn_branch[l].astype(BF16), w_out[l].astype(BF16), _tile(S, 512))

    h2f = h2.reshape(T, D)
    keys = peer_sub_keys[l].reshape(2 * PEER_HEADS, PEER_KEYS, PEER_HALF).astype(BF16)
    rb, e2, cnt, e1 = _route_call(h2f, w_peer_q[l].T.astype(BF16), keys, _tile(T, 256))
    tp = _tile(S, 512)
    out = _peer_call(h2f, peer_u[l].astype(BF16), peer_v[l].T.astype(BF16), rb, e2, cnt, e1,
                     x1.reshape(T, D), gt2, g_post_ffn[l].reshape(D, 1), tp, 512, S)
    return out.reshape(B, S, D)
```

```python
import functools
import math

import jax
import jax.numpy as jnp
from jax import lax
from jax.experimental import pallas as pl
from jax.experimental.pallas import tpu as pltpu

F32 = jnp.float32
BF16 = jnp.bfloat16

ATTN_HEADS = 8
HEAD_DIM = 64
V_DIM = 2 * HEAD_DIM
ROPE_HALF = 8
ROPE_THETA = 500000.0
GMLP_GROUPS = 8
GMLP_CHUNK = 128
PEER_HEADS = 8
PEER_KEYS = 128
PEER_TOPK = 16
PEER_HALF = 128
NORM_EPS = 1e-6
N_MOD = 6
LAMBDA_INIT = 0.8 - 0.6 * math.exp(-0.3 * 0)
Q_SCALE = HEAD_DIM ** -0.5 * math.log2(math.e)

LANES = 128
VMEM_LIMIT = 56 * 1024 * 1024

NT_DIMS = (((1,), (1,)), ((), ()))


def _params(sem):
    return pltpu.CompilerParams(dimension_semantics=sem, vmem_limit_bytes=VMEM_LIMIT)


def _gelu_tanh(x):
    return 0.5 * x * (1.0 + jnp.tanh(0.7978845608028654 * (x + 0.044715 * (x * x * x))))


def _sigmoid(x):
    return 1.0 / (1.0 + jnp.exp(-x))


def _rms(x, axis):
    return x * lax.rsqrt(jnp.mean(x * x, axis=axis, keepdims=True) + NORM_EPS)


def _mod_kernel(c_ref, w_ref, b_ref, o_ref):
    c = c_ref[...]
    ca = c * _sigmoid(c)
    o_ref[...] = jnp.dot(ca, w_ref[...], preferred_element_type=F32,
                         precision=lax.Precision.HIGHEST) + b_ref[...]


def _mod_call(c, w_ada, b_ada):
    B, D = c.shape
    return pl.pallas_call(
        _mod_kernel,
        grid=(N_MOD,),
        in_specs=[pl.BlockSpec((B, D), lambda j: (0, 0)),
                  pl.BlockSpec((D, D), lambda j: (0, j)),
                  pl.BlockSpec((1, D), lambda j: (0, j))],
        out_specs=pl.BlockSpec((B, D), lambda j: (0, j)),
        out_shape=jax.ShapeDtypeStruct((B, N_MOD * D), F32),
        compiler_params=_params(("arbitrary",)),
        name="mod",
    )(c, w_ada, b_ada.reshape(1, N_MOD * D))


def _prenorm(x_ref, g_ref, sc_ref, sh_ref):
    x = x_ref[0]
    return (_rms(x, -1) * g_ref[...] * (1.0 + sc_ref[0]) + sh_ref[0]).astype(BF16)


def _qkv_kernel(x_ref, g_ref, sc_ref, sh_ref, wq_ref, wk_ref, wv_ref, cos_ref, sin_ref,
                kc_ref, ka_ref, kb_ref, q_out, k_out, v_out):
    hb = _prenorm(x_ref, g_ref, sc_ref, sh_ref)
    qt = lax.dot_general(wq_ref[...], hb, NT_DIMS, preferred_element_type=F32)
    cos = cos_ref[0]
    sin = sin_ref[0]
    pieces = []
    for g in range(2 * ATTN_HEADS):
        base = g * HEAD_DIM
        r1 = qt[base:base + ROPE_HALF]
        r2 = qt[base + ROPE_HALF:base + 2 * ROPE_HALF]
        pieces += [r1 * cos - r2 * sin, r2 * cos + r1 * sin, qt[base + 2 * ROPE_HALF:base + HEAD_DIM]]
    q_out[0] = (jnp.concatenate(pieces, axis=0) * Q_SCALE).astype(BF16)
    kf = jnp.dot(hb, wk_ref[...], preferred_element_type=F32)
    kc, ka, kb = kc_ref[0], ka_ref[0], kb_ref[0]
    outs = []
    for c in range(ATTN_HEADS):
        t = kf[:, c * LANES:(c + 1) * LANES]
        outs.append(t * kc + pltpu.roll(t, ROPE_HALF, 1) * ka + pltpu.roll(t, LANES - ROPE_HALF, 1) * kb)
    k_out[0] = jnp.concatenate(outs, axis=1).astype(BF16)
    v_out[0] = lax.dot_general(wv_ref[...], hb, NT_DIMS, preferred_element_type=F32).astype(BF16)


def _qkv_call(x, g_pre, sc1, sh1, wq_t, wk, wv_t, cos_t, sin_t, kc, ka, kb, tm):
    B, S, D = x.shape
    C = ATTN_HEADS * V_DIM
    tok = lambda b, i: (b, i, 0)
    per_b = lambda b, i: (b, 0, 0)
    const = lambda b, i: (0, 0)
    tr = lambda b, i: (b, 0, i)
    return pl.pallas_call(
        _qkv_kernel,
        grid=(B, S // tm),
        in_specs=[pl.BlockSpec((1, tm, D), tok),
                  pl.BlockSpec((1, D), const),
                  pl.BlockSpec((1, 1, D), per_b),
                  pl.BlockSpec((1, 1, D), per_b),
                  pl.BlockSpec((C, D), const),
                  pl.BlockSpec((D, C), const),
                  pl.BlockSpec((C, D), const),
                  pl.BlockSpec((1, ROPE_HALF, tm), tr),
                  pl.BlockSpec((1, ROPE_HALF, tm), tr),
                  pl.BlockSpec((1, tm, LANES), tok),
                  pl.BlockSpec((1, tm, LANES), tok),
                  pl.BlockSpec((1, tm, LANES), tok)],
        out_specs=[pl.BlockSpec((1, C, tm), tr),
                   pl.BlockSpec((1, tm, C), tok),
                   pl.BlockSpec((1, C, tm), tr)],
        out_shape=[jax.ShapeDtypeStruct((B, C, S), BF16),
                   jax.ShapeDtypeStruct((B, S, C), BF16),
                   jax.ShapeDtypeStruct((B, C, S), BF16)],
        compiler_params=_params(("parallel", "parallel")),
        name="qkv",
    )(x, g_pre, sc1, sh1, wq_t, wk, wv_t, cos_t, sin_t, kc, ka, kb)


def _gmlp_kernel(x_ref, g_ref, sc_ref, sh_ref, wz_ref, wg_ref, lng_ref, lnb_ref, ws_ref, bs_ref,
                 wo_ref, ga_out, gy_out):
    hb = _prenorm(x_ref, g_ref, sc_ref, sh_ref)
    tm = hb.shape[0]
    W = lng_ref.shape[1]
    gz = _gelu_tanh(jnp.dot(hb, wz_ref[...], preferred_element_type=F32))
    u = gz[:, :W]
    v = gz[:, W:]
    mu = jnp.mean(v, axis=-1, keepdims=True)
    vc = v - mu
    var = jnp.mean(vc * vc, axis=-1, keepdims=True)
    vn = (vc * lax.rsqrt(var + NORM_EPS) * lng_ref[...] + lnb_ref[...]).astype(BF16)
    gd = W // GMLP_GROUPS
    rows = []
    for ch in range(tm // GMLP_CHUNK):
        r0 = ch * GMLP_CHUNK
        cols = [jnp.dot(ws_ref[g], vn[r0:r0 + GMLP_CHUNK, g * gd:(g + 1) * gd],
                        preferred_element_type=F32) for g in range(GMLP_GROUPS)]
        rows.append(jnp.concatenate(cols, axis=1) + bs_ref[...])
    sv = jnp.concatenate(rows, axis=0)
    sg = (u * sv).astype(BF16)
    yg = jnp.dot(sg, wo_ref[...], preferred_element_type=F32)
    gates = _sigmoid(jnp.dot(hb, wg_ref[...], preferred_element_type=F32))
    D = yg.shape[1]
    ga_out[0] = gates[:, :D].astype(BF16)
    gy_out[0] = (gates[:, D:] * yg).astype(BF16)


def _gmlp_call(x, g_pre, sc1, sh1, wz, wg, ln_g, ln_b, ws, bs, wo, tm):
    B, S, D = x.shape
    W = ln_g.shape[1]
    tok = lambda b, i: (b, i, 0)
    per_b = lambda b, i: (b, 0, 0)
    const = lambda b, i: (0, 0)
    return pl.pallas_call(
        _gmlp_kernel,
        grid=(B, S // tm),
        in_specs=[pl.BlockSpec((1, tm, D), tok),
                  pl.BlockSpec((1, D), const),
                  pl.BlockSpec((1, 1, D), per_b),
                  pl.BlockSpec((1, 1, D), per_b),
                  pl.BlockSpec((D, 2 * W), const),
                  pl.BlockSpec((D, 2 * D), const),
                  pl.BlockSpec((1, W), const),
                  pl.BlockSpec((1, W), const),
                  pl.BlockSpec((GMLP_GROUPS, GMLP_CHUNK, GMLP_CHUNK), lambda b, i: (0, 0, 0)),
                  pl.BlockSpec((GMLP_CHUNK, W), const),
                  pl.BlockSpec((W, D), const)],
        out_specs=[pl.BlockSpec((1, tm, D), tok), pl.BlockSpec((1, tm, D), tok)],
        out_shape=[jax.ShapeDtypeStruct((B, S, D), BF16), jax.ShapeDtypeStruct((B, S, D), BF16)],
        compiler_params=_params(("parallel", "parallel")),
        name="gmlp",
    )(x, g_pre, sc1, sh1, wz, wg, ln_g, ln_b, ws, bs, wo)


def _attn_kernel(lam_ref, q_ref, k_ref, v_ref, gs_ref, o_ref, acc_ref, qz_ref, sa_ref, sb_ref, *, tk):
    S = k_ref.shape[1]
    nblk = S // tk
    nchain = q_ref.shape[2] // LANES
    lp = lam_ref[...]
    lam = (jnp.exp(jnp.sum(lp[0:1] * lp[1:2], axis=1, keepdims=True))
           - jnp.exp(jnp.sum(lp[2:3] * lp[3:4], axis=1, keepdims=True)) + LAMBDA_INIT)
    row = lax.broadcasted_iota(jnp.int32, (V_DIM, LANES), 0)
    for c in range(nchain):
        qc = q_ref[0, :, c * LANES:(c + 1) * LANES].astype(F32)
        qz_ref[c] = jnp.concatenate([jnp.where(row < HEAD_DIM, qc, 0.0),
                                     jnp.where(row >= HEAD_DIM, qc, 0.0)], axis=1).astype(BF16)
    acc_ref[...] = jnp.zeros_like(acc_ref)
    k0 = k_ref[0, 0:tk, :]
    for c in range(nchain):
        sa_ref[c] = jnp.dot(k0, qz_ref[c], preferred_element_type=F32)

    def half_step(j, cur_ref, nxt_ref, ms, ls):
        offn = pl.multiple_of(jnp.minimum(j + 1, nblk - 1) * tk, tk)
        kn = k_ref[0, pl.ds(offn, tk), :]
        vb = v_ref[0, :, pl.ds(pl.multiple_of(j * tk, tk), tk)]
        new_m, new_l = [], []
        for c in range(nchain):
            nxt_ref[c] = jnp.dot(kn, qz_ref[c], preferred_element_type=F32)
            s = cur_ref[c]
            mn = jnp.maximum(ms[c], jnp.max(s, axis=0, keepdims=True))
            alpha = jnp.exp2(ms[c] - mn)
            p = jnp.exp2(s - mn)
            new_l.append(alpha * ls[c] + jnp.sum(p, axis=0, keepdims=True))
            new_m.append(mn)
            acc_ref[c] = alpha * acc_ref[c] + jnp.dot(vb, p.astype(BF16), preferred_element_type=F32)
        return tuple(new_m), tuple(new_l)

    def body(jj, carry):
        ms, ls = carry
        ms, ls = half_step(2 * jj, sa_ref, sb_ref, ms, ls)
        return half_step(2 * jj + 1, sb_ref, sa_ref, ms, ls)

    ninf = tuple(jnp.full((1, 2 * LANES), -jnp.inf, F32) for _ in range(nchain))
    zero = tuple(jnp.zeros((1, 2 * LANES), F32) for _ in range(nchain))
    _, ls = lax.fori_loop(0, nblk // 2, body, (ninf, zero))
    for c in range(nchain):
        a = acc_ref[c] / ls[c]
        o = a[:, :LANES] - lam * a[:, LANES:]
        on = _rms(o, 0) * gs_ref[...] * (1.0 - LAMBDA_INIT)
        o_ref[0, c * LANES:(c + 1) * LANES, :] = on.T.astype(BF16)


def _attn_call(lam_p, q_t, k, v_t, g_subln, tq, tk):
    B, S, C = k.shape
    return pl.pallas_call(
        functools.partial(_attn_kernel, tk=tk),
        grid=(B, ATTN_HEADS, S // tq),
        in_specs=[pl.BlockSpec((4, HEAD_DIM), lambda b, h, i: (0, 0)),
                  pl.BlockSpec((1, V_DIM, tq), lambda b, h, i: (b, h, i)),
                  pl.BlockSpec((1, S, V_DIM), lambda b, h, i: (b, 0, h)),
                  pl.BlockSpec((1, V_DIM, S), lambda b, h, i: (b, h, 0)),
                  pl.BlockSpec((V_DIM, 1), lambda b, h, i: (0, 0))],
        out_specs=pl.BlockSpec((1, tq, V_DIM), lambda b, h, i: (b, i, h)),
        out_shape=jax.ShapeDtypeStruct((B, S, C), BF16),
        scratch_shapes=[pltpu.VMEM((tq // LANES, V_DIM, 2 * LANES), F32),
                        pltpu.VMEM((tq // LANES, V_DIM, 2 * LANES), BF16),
                        pltpu.VMEM((tq // LANES, tk, 2 * LANES), F32),
                        pltpu.VMEM((tq // LANES, tk, 2 * LANES), F32)],
        compiler_params=_params(("parallel", "parallel", "parallel")),
        name="attn",
    )(lam_p, q_t, k, v_t, g_subln)


def _merge_kernel(o_ref, ga_ref, gy_ref, x_ref, gt1_ref, sc2_ref, sh2_ref, gpm_ref, gpf_ref,
                  wa_ref, wo_ref, x1_out, h2_out):
    ya = jnp.dot(o_ref[0], wa_ref[...], preferred_element_type=F32)
    mixin = (ga_ref[0].astype(F32) * ya + gy_ref[0].astype(F32)).astype(BF16)
    mix = jnp.dot(mixin, wo_ref[...], preferred_element_type=F32)
    x1 = x_ref[0] + gt1_ref[0] * (_rms(mix, -1) * gpm_ref[...])
    x1_out[0] = x1
    h2 = _rms(x1, -1) * gpf_ref[...] * (1.0 + sc2_ref[0]) + sh2_ref[0]
    h2_out[...] = h2.T.astype(BF16)


def _merge_call(o, ga, gy, x, gt1, sc2, sh2, g_post_mix, g_pre_ffn, wa, wo, tm):
    B, S, D = x.shape
    C = o.shape[2]
    tok = lambda b, i: (b, i, 0)
    per_b = lambda b, i: (b, 0, 0)
    const = lambda b, i: (0, 0)
    return pl.pallas_call(
        _merge_kernel,
        grid=(B, S // tm),
        in_specs=[pl.BlockSpec((1, tm, C), tok),
                  pl.BlockSpec((1, tm, D), tok),
                  pl.BlockSpec((1, tm, D), tok),
                  pl.BlockSpec((1, tm, D), tok),
                  pl.BlockSpec((1, 1, D), per_b),
                  pl.BlockSpec((1, 1, D), per_b),
                  pl.BlockSpec((1, 1, D), per_b),
                  pl.BlockSpec((1, D), const),
                  pl.BlockSpec((1, D), const),
                  pl.BlockSpec((C, D), const),
                  pl.BlockSpec((D, D), const)],
        out_specs=[pl.BlockSpec((1, tm, D), tok),
                   pl.BlockSpec((D, tm), lambda b, i: (0, b * (S // tm) + i))],
        out_shape=[jax.ShapeDtypeStruct((B, S, D), F32), jax.ShapeDtypeStruct((D, B * S), BF16)],
        compiler_params=_params(("parallel", "parallel")),
        name="merge",
    )(o, ga, gy, x, gt1, sc2, sh2, g_post_mix, g_pre_ffn, wa, wo)


def _oddeven_merge(lo, hi, r):
    step = r * 2
    if step < hi - lo:
        yield from _oddeven_merge(lo, hi, step)
        yield from _oddeven_merge(lo + r, hi, step)
        yield from [(i, i + r) for i in range(lo + r, hi - r, step)]
    else:
        yield (lo, lo + r)


def _oddeven_sort(lo, hi):
    if hi - lo >= 1:
        mid = lo + (hi - lo) // 2
        yield from _oddeven_sort(lo, mid)
        yield from _oddeven_sort(mid + 1, hi)
        yield from _oddeven_merge(lo, hi, 1)


def _bitonic_merge(n):
    d = n // 2
    while d >= 1:
        yield from [(i, i + d) for i in range(n) if (i & d) == 0]
        d //= 2


SORT16 = tuple(_oddeven_sort(0, PEER_TOPK - 1))
MERGE16 = tuple(_bitonic_merge(PEER_TOPK))


def _vmax(a, b):
    if a is None:
        return b
    if b is None:
        return a
    return jnp.maximum(a, b)


def _vmin(a, b):
    if a is None or b is None:
        return None
    return jnp.minimum(a, b)


def _run_network(xs, net):
    for i, j in net:
        xs[i], xs[j] = _vmax(xs[i], xs[j]), _vmin(xs[i], xs[j])
    return xs


def _fold_sublanes(xs, sh):
    n = len(xs)
    ys = [None if x is None else pltpu.roll(x, sh, 0) for x in xs]
    return [_vmax(xs[v], ys[n - 1 - v]) for v in range(n)]


def _top16(xs):
    xs = _run_network(list(xs), SORT16)
    for sh in (4, 2, 1):
        xs = _run_network(_fold_sublanes(xs, sh), MERGE16)
    return xs


def _kth16(xs):
    xs = _run_network(list(xs), SORT16)
    for sh in (4, 2):
        xs = _run_network(_fold_sublanes(xs, sh), MERGE16)
    xs = _fold_sublanes(xs, 1)
    out = xs[0]
    for x in xs[1:]:
        out = jnp.minimum(out, x)
    return out


def _sum_sublanes(x):
    for sh in (4, 2, 1):
        x = x + pltpu.roll(x, sh, 0)
    return x


def _dup_bf16(x):
    bits = lax.bitcast_convert_type(x.astype(BF16).astype(F32), jnp.uint32)
    return bits | (bits >> 16)


def _route_kernel(h2_ref, wq_ref, keys_ref, rb_out, e2_out, cnt_out, e1_out, s_ref):
    tm = h2_ref.shape[1]
    qt = jnp.dot(wq_ref[...], h2_ref[...], preferred_element_type=F32).astype(BF16)
    for c in range(2 * PEER_HEADS):
        s_ref[c] = jnp.dot(keys_ref[c], qt[c * PEER_HALF:(c + 1) * PEER_HALF],
                           preferred_element_type=F32)

    K = PEER_TOPK
    NV = PEER_KEYS // 8
    sub = lax.broadcasted_iota(jnp.int32, (8, LANES), 0)
    ninf = -jnp.inf

    def pack(vs):
        p = vs[0]
        for r in range(1, 8):
            p = jnp.where(sub == r, vs[r], p)
        return p

    for lb in range(tm // LANES):
        lanes = slice(lb * LANES, (lb + 1) * LANES)

        def head_body(h, carry, lanes=lanes):
            x1 = [s_ref[2 * h, 8 * v:8 * v + 8, lanes] for v in range(NV)]
            x2 = [s_ref[2 * h + 1, 8 * v:8 * v + 8, lanes] for v in range(NV)]
            v1 = _top16(x1)
            v2 = _top16(x2)
            p1a, p1b, p2b = pack(v1[0:8]), pack(v1[8:16]), pack(v2[8:16])
            cands = [p1a + v2[0], p1b + v2[0]]
            for b in range(1, 8):
                cands.append(jnp.where(sub < K // (b + 1), p1a + v2[b], ninf))
            cands.append(v1[0] + p2b)
            tau = _kth16(cands + [None] * (K - len(cands)))
            top = v1[0] + v2[0]
            sel = [(c >= tau).astype(F32) for c in cands]
            z = sel[0] * jnp.exp(cands[0] - top)
            for s, c in zip(sel[1:], cands[1:]):
                z = z + s * jnp.exp(c - top)
            inv_z = 0.5 / _sum_sublanes(z)
            cnt_lo = sel[0]
            for b in range(1, 8):
                cnt_lo = cnt_lo + sel[1 + b]
            cnt_lo = cnt_lo + jnp.where(sub == 0, _sum_sublanes(sel[9]), 0.0)
            cnt_hi = sel[1]
            cnt_a = ([jnp.broadcast_to(cnt_lo[a:a + 1], (8, LANES)) for a in range(8)]
                     + [jnp.broadcast_to(cnt_hi[a:a + 1], (8, LANES)) for a in range(8)])
            rb, e2, cnt, e1 = [], [], [], []
            for v in range(NV):
                r = jnp.full((8, LANES), float(K), F32)
                c = jnp.zeros((8, LANES), F32)
                for a in range(K - 1, -1, -1):
                    r = jnp.where(x2[v] >= v2[a], float(a), r)
                    c = jnp.where(x1[v] >= v1[a], cnt_a[a], c)
                rb.append(r)
                cnt.append(c)
                e2.append(jnp.exp(x2[v] - v2[0]))
                e1.append(jnp.exp(x1[v] - v1[0]) * inv_z)
            rb_out[h, :, lanes] = jnp.concatenate(rb, axis=0).astype(BF16)
            e2_out[h, :, lanes] = jnp.concatenate(e2, axis=0).astype(BF16)
            cnt_out[h, :, lanes] = _dup_bf16(jnp.concatenate(cnt, axis=0))
            e1_out[h, :, lanes] = _dup_bf16(jnp.concatenate(e1, axis=0))
            return carry

        lax.fori_loop(0, PEER_HEADS, head_body, 0)


def _route_call(h2_t, wq_t, keys, tm):
    D, T = h2_t.shape
    Q = wq_t.shape[0]
    shp = lambda dt: jax.ShapeDtypeStruct((PEER_HEADS, PEER_KEYS, T), dt)
    ospec = pl.BlockSpec((PEER_HEADS, PEER_KEYS, tm), lambda i: (0, 0, i))
    return pl.pallas_call(
        _route_kernel,
        grid=(T // tm,),
        in_specs=[pl.BlockSpec((D, tm), lambda i: (0, i)),
                  pl.BlockSpec((Q, D), lambda i: (0, 0)),
                  pl.BlockSpec((2 * PEER_HEADS, PEER_KEYS, PEER_HALF), lambda i: (0, 0, 0))],
        out_specs=[ospec, ospec, ospec, ospec],
        out_shape=[shp(BF16), shp(BF16), shp(jnp.uint32), shp(jnp.uint32)],
        scratch_shapes=[pltpu.VMEM((2 * PEER_HEADS, PEER_KEYS, tm), F32)],
        compiler_params=_params(("parallel",)),
        name="route",
    )(h2_t, wq_t, keys)


def _peer_kernel(h2_ref, u_ref, vt_ref, rb_ref, e2_ref, cnt_ref, e1_ref, x1_ref, gt2_ref, g_ref,
                 o_ref, acc_ref, w_ref, *, eb):
    e = pl.program_id(1)

    @pl.when(e == 0)
    def _():
        acc_ref[...] = jnp.zeros_like(acc_ref)

    tm = h2_ref.shape[1]
    te = u_ref.shape[0]
    zero = jnp.zeros((), BF16)
    one = jnp.ones((), BF16)
    c0 = jnp.asarray(0.7978845608028654, BF16)
    c1 = jnp.asarray(0.035677408136300125, BF16)
    slab = 16
    for blk in range(te // eb):
        at = jnp.dot(u_ref[blk * eb:(blk + 1) * eb, :], h2_ref[...],
                     preferred_element_type=F32).astype(BF16)
        act = at * (one + jnp.tanh(at * (c0 + c1 * (at * at))))
        for ii in range(eb // PEER_KEYS):
            i = blk * (eb // PEER_KEYS) + ii
            cs = [pltpu.bitcast(jnp.broadcast_to(cnt_ref[h, i:i + 1, :], (8, tm)), BF16)
                  for h in range(PEER_HEADS)]
            ws = [pltpu.bitcast(jnp.broadcast_to(e1_ref[h, i:i + 1, :], (8, tm)), BF16)
                  for h in range(PEER_HEADS)]
            for r in range(PEER_KEYS // slab):
                rows = slice(r * slab, (r + 1) * slab)
                g = None
                for h in range(PEER_HEADS):
                    term = jnp.where(rb_ref[h, rows, :] < cs[h], e2_ref[h, rows, :], zero) * ws[h]
                    g = term if g is None else g + term
                a0 = ii * PEER_KEYS + r * slab
                w_ref[blk * eb + a0:blk * eb + a0 + slab, :] = act[a0:a0 + slab] * g
    acc_ref[...] += jnp.dot(vt_ref[...], w_ref[...], preferred_element_type=F32)

    @pl.when(e == pl.num_programs(1) - 1)
    def _():
        yn = _rms(acc_ref[...], 0) * g_ref[...]
        o_ref[...] = x1_ref[...] + gt2_ref[0] * yn.T


def _peer_call(h2_t, u, v_t, rb, e2, cnt, e1, x1, gt2, g_post, tm, te, eb, S):
    D, T = h2_t.shape
    E = u.shape[0]
    ni = te // PEER_KEYS
    rspec = pl.BlockSpec((PEER_HEADS, PEER_KEYS, tm), lambda t, e: (0, 0, t))
    ispec = pl.BlockSpec((PEER_HEADS, ni, tm), lambda t, e: (0, e, t))
    return pl.pallas_call(
        functools.partial(_peer_kernel, eb=eb),
        grid=(T // tm, E // te),
        in_specs=[pl.BlockSpec((D, tm), lambda t, e: (0, t)),
                  pl.BlockSpec((te, D), lambda t, e: (e, 0)),
                  pl.BlockSpec((D, te), lambda t, e: (0, e)),
                  rspec, rspec, ispec, ispec,
                  pl.BlockSpec((tm, D), lambda t, e: (t, 0)),
                  pl.BlockSpec((1, 1, D), lambda t, e: ((t * tm) // S, 0, 0)),
                  pl.BlockSpec((D, 1), lambda t, e: (0, 0))],
        out_specs=pl.BlockSpec((tm, D), lambda t, e: (t, 0)),
        out_shape=jax.ShapeDtypeStruct((T, D), F32),
        scratch_shapes=[pltpu.VMEM((D, tm), F32), pltpu.VMEM((te, tm), BF16)],
        compiler_params=_params(("parallel", "arbitrary")),
        name="peer",
    )(h2_t, u, v_t, rb, e2, cnt, e1, x1, gt2, g_post)


def _rope_tables(positions):
    inv_freq = ROPE_THETA ** (-jnp.arange(0, 2 * ROPE_HALF, 2, dtype=F32) / (2 * ROPE_HALF))
    ang = positions.astype(F32)[..., None] * inv_freq
    cos, sin = jnp.cos(ang), jnp.sin(ang)
    lane = jnp.arange(LANES)
    d = lane % HEAD_DIM
    f = d % ROPE_HALF
    kc = jnp.where(d < 2 * ROPE_HALF, cos[..., f], 1.0)
    ka = jnp.where((d >= ROPE_HALF) & (d < 2 * ROPE_HALF), sin[..., f], 0.0)
    kb = jnp.where(d < ROPE_HALF, -sin[..., f], 0.0)
    return cos.transpose(0, 2, 1), sin.transpose(0, 2, 1), kc, ka, kb


def _tile(n, pref):
    t = min(n, pref)
    assert n % t == 0, (n, t)
    return t


def kernel(x, c, positions, w_ada, b_ada, g_pre_mix, g_post_mix, w_in, lambda_q1, lambda_k1, lambda_q2, lambda_k2, g_subln, w_attn_branch, gmlp_ln_g, gmlp_ln_b, w_spatial, b_spatial, w_gmlp_branch, w_out, g_pre_ffn, g_post_ffn, w_peer_q, peer_sub_keys, peer_u, peer_v):
    B, S, D = x.shape
    depth = w_ada.shape[0]
    assert depth == 1 and S % GMLP_CHUNK == 0
    C = ATTN_HEADS * V_DIM
    W = gmlp_ln_g.shape[1]
    T = B * S
    cos_t, sin_t, kc, ka, kb = _rope_tables(positions)
    l = 0
    mod = _mod_call(c, w_ada[l], b_ada[l])
    sh1, sc1, gt1, sh2, sc2, gt2 = [m[:, None, :] for m in jnp.split(mod, N_MOD, axis=-1)]

    wi = w_in[l].astype(BF16)
    wq_t = wi[:, :C].T
    wk = wi[:, C:2 * C]
    wv_t = wi[:, 2 * C:3 * C].T
    wz = wi[:, 3 * C:3 * C + 2 * W]
    wg = wi[:, 3 * C + 2 * W:]
    row = lambda a: a.reshape(1, -1)

    tm = _tile(S, 256)
    q_t, k, v_t = _qkv_call(x, row(g_pre_mix[l]), sc1, sh1, wq_t, wk, wv_t, cos_t, sin_t, kc, ka, kb, tm)
    bs = jnp.repeat(b_spatial[l].T, W // GMLP_GROUPS, axis=1)
    ga, gy = _gmlp_call(x, row(g_pre_mix[l]), sc1, sh1, wz, wg, row(gmlp_ln_g[l]), row(gmlp_ln_b[l]),
                        w_spatial[l].astype(BF16), bs, w_gmlp_branch[l].astype(BF16), tm)

    lam_p = jnp.stack([lambda_q1[l], lambda_k1[l], lambda_q2[l], lambda_k2[l]])
    tk = _tile(S // 2, 256)
    assert (S // tk) % 2 == 0
    o = _attn_call(lam_p, q_t, k, v_t, g_subln[l].reshape(V_DIM, 1), _tile(S, 512), tk)

    x1, h2_t = _merge_call(o, ga, gy, x, gt1, sc2, sh2, row(g_post_mix[l]), row(g_pre_ffn[l]),
                           w_attn_branch[l].astype(BF16), w_out[l].astype(BF16), _tile(S, 512))

    keys = peer_sub_keys[l].reshape(2 * PEER_HEADS, PEER_KEYS, PEER_HALF).astype(BF16)
    rb, e2, cnt, e1 = _route_call(h2_t, w_peer_q[l].T.astype(BF16), keys, _tile(T, 256))
    tp = _tile(S, 1024)
    out = _peer_call(h2_t, peer_u[l].astype(BF16), peer_v[l].T.astype(BF16), rb, e2, cnt, e1,
                     x1.reshape(T, D), gt2, g_post_ffn[l].reshape(D, 1), tp, 1024, 256, S)
    return out.reshape(B, S, D)
```
